```python
import math
import jax, jax.numpy as jnp
from jax import lax
import numpy as np

D_MODEL = 2048
BATCH = 4
SEQ = 2048
DEPTH = 4

GRID_W = 64
CTX_LEN = 256
N_MIXERS = 2
N_RET_LAYERS = (DEPTH + 1) // 2
N_ATTN_LAYERS = DEPTH // 2
RET_HEADS = 8
RET_DK = D_MODEL // RET_HEADS
RET_DV = 2 * RET_DK
RET_CHUNK = 128
RET_QK_W = RET_HEADS * RET_DK
RET_V_W = RET_HEADS * RET_DV
RET_IN = 2 * RET_QK_W + 2 * RET_V_W
ATTN_HEADS = 16
ATTN_KV_HEADS = 4
ATTN_HEAD_DIM = D_MODEL // ATTN_HEADS
ATTN_GROUP = ATTN_HEADS // ATTN_KV_HEADS
WINDOW = 128
ATTN_BLOCK = 128
ATTN_Q_W = ATTN_HEADS * ATTN_HEAD_DIM
ATTN_KV_W = ATTN_KV_HEADS * ATTN_HEAD_DIM
ATTN_IN = 2 * ATTN_Q_W + 2 * ATTN_KV_W
ROPE_BASE = 10000.0
EPS = 1e-6
NEG_INF = -1e30

kernel_name = 'hybrid_retention_swa_prefix_dit'


def _rms_norm(x, gain):
    xf = x.astype(jnp.float32)
    y = xf * lax.rsqrt(jnp.mean(xf * xf, axis=-1, keepdims=True) + EPS)
    return (y * gain.astype(jnp.float32)).astype(x.dtype)


def _rope_axis(x, pos):
    n2 = x.shape[-1] // 2
    inv = ROPE_BASE ** (-jnp.arange(n2, dtype=jnp.float32) / n2)
    ang = pos.astype(jnp.float32)[:, None] * inv[None, :]
    cos, sin = jnp.cos(ang), jnp.sin(ang)
    xf = x.astype(jnp.float32)
    x1, x2 = xf[..., :n2], xf[..., n2:]
    return jnp.concatenate([x1 * cos - x2 * sin, x1 * sin + x2 * cos], axis=-1).astype(x.dtype)


def _rope_2d(x, row, col):
    half = x.shape[-1] // 2
    return jnp.concatenate([_rope_axis(x[..., :half], row), _rope_axis(x[..., half:], col)], axis=-1)


def _heads(t, n, d):
    b, l, _ = t.shape
    return t.reshape(b, l, n, d).transpose(0, 2, 1, 3)


def _retention_chunked(q, k, v, log_gamma, s0, inclusive):
    b, h, l, dk = q.shape
    dv = v.shape[-1]
    n = l // RET_CHUNK
    qc = q.reshape(b, h, n, RET_CHUNK, dk)
    kc = k.reshape(b, h, n, RET_CHUNK, dk)
    vc = v.reshape(b, h, n, RET_CHUNK, dv)
    idx = jnp.arange(RET_CHUNK, dtype=jnp.float32)
    diff = idx[:, None] - idx[None, :]
    allowed = (diff >= 0) if inclusive else (diff > 0)
    lg = log_gamma[:, None, None]
    decay = jnp.where(allowed[None], jnp.exp(lg * jnp.maximum(diff, 0.0)[None]), 0.0)
    scores = jnp.einsum('bhnid,bhnjd->bhnij', qc, kc) * decay[None, :, None]
    inner = jnp.einsum('bhnij,bhnjv->bhniv', scores, vc)
    q_w = jnp.exp(log_gamma[:, None] * (idx + 1.0)[None])[None, :, None, :, None]
    k_w = jnp.exp(log_gamma[:, None] * (RET_CHUNK - 1.0 - idx)[None])[None, :, None, :, None]
    chunk_decay = jnp.exp(log_gamma * RET_CHUNK)[None, :, None, None]
    xs = (jnp.moveaxis(qc * q_w, 2, 0), jnp.moveaxis(kc * k_w, 2, 0), jnp.moveaxis(vc, 2, 0))

    def step(state, blk):
        qb, kb, vb = blk
        out = jnp.einsum('bhid,bhdv->bhiv', qb, state)
        state = chunk_decay * state + jnp.einsum('bhjd,bhjv->bhdv', kb, vb)
        return state, out

    s_fin, cross = lax.scan(step, s0, xs)
    out = inner + jnp.moveaxis(cross, 0, 2)
    return out.reshape(b, h, l, dv), s_fin


def _retention_bidir(q, k, v, lg_f, lg_b, s0_f, s0_b):
    o_f, s_f = _retention_chunked(q, k, v, lg_f, s0_f, True)
    flip = lambda t: jnp.flip(t, axis=2)
    o_b, s_b = _retention_chunked(flip(q), flip(k), flip(v), lg_b, s0_b, False)
    return o_f + flip(o_b), s_f, s_b


def _retention_mixer(h_lat, h_ctx, w_in, logit_f, logit_b, norm_gain, w_out, row, col, with_ctx_out):
    lg_f = jax.nn.log_sigmoid(logit_f.astype(jnp.float32))
    lg_b = jax.nn.log_sigmoid(logit_b.astype(jnp.float32))

    def project(h):
        p = h @ w_in
        q = _heads(p[..., :RET_QK_W], RET_HEADS, RET_DK).astype(jnp.float32) * (RET_DK ** -0.5)
        k = _heads(p[..., RET_QK_W:2 * RET_QK_W], RET_HEADS, RET_DK).astype(jnp.float32)
        v = _heads(p[..., 2 * RET_QK_W:2 * RET_QK_W + RET_V_W], RET_HEADS, RET_DV).astype(jnp.float32)
        g = p[..., 2 * RET_QK_W + RET_V_W:]
        return q, k, v, g

    def finish(o, g):
        b, h, l, dv = o.shape
        on = o * lax.rsqrt(jnp.mean(o * o, axis=-1, keepdims=True) + EPS)
        on = on.transpose(0, 2, 1, 3).reshape(b, l, h * dv) * norm_gain.astype(jnp.float32)
        return (on.astype(g.dtype) * jax.nn.silu(g)) @ w_out

    q_c, k_c, v_c, g_c = project(h_ctx)
    q_l, k_l, v_l, g_l = project(h_lat)
    q_l = _rope_2d(q_l, row, col)
    k_l = _rope_2d(k_l, row, col)
    b = h_lat.shape[0]
    zeros = jnp.zeros((b, RET_HEADS, RET_DK, RET_DV), jnp.float32)
    o_c, s_f, s_b = _retention_bidir(q_c, k_c, v_c, lg_f, lg_b, zeros, zeros)
    o_l, _, _ = _retention_bidir(q_l, k_l, v_l, lg_f, lg_b, s_f, s_b)
    y_l = finish(o_l.astype(jnp.float32), g_l).astype(h_lat.dtype)
    y_c = finish(o_c, g_c).astype(h_ctx.dtype) if with_ctx_out else None
    return y_l, y_c


def _attn_out(o, g, w_out):
    b, _, _, l, _ = o.shape
    o = o.transpose(0, 3, 1, 2, 4).reshape(b, l, ATTN_Q_W)
    return (o * jax.nn.silu(g)) @ w_out


def _attention_mixer(h_lat, h_ctx, w_in, q_gain, k_gain, sink, w_out, row, col, with_ctx_out):
    qd, kvd = ATTN_Q_W, ATTN_KV_W
    scale = ATTN_HEAD_DIM ** -0.5
    sink_g = sink.astype(jnp.float32).reshape(ATTN_KV_HEADS, ATTN_GROUP)
    p_l = h_lat @ w_in
    q_l = _rms_norm(_heads(p_l[..., :qd], ATTN_HEADS, ATTN_HEAD_DIM), q_gain)
    k_l = _rms_norm(_heads(p_l[..., qd:qd + kvd], ATTN_KV_HEADS, ATTN_HEAD_DIM), k_gain)
    v_l = _heads(p_l[..., qd + kvd:qd + 2 * kvd], ATTN_KV_HEADS, ATTN_HEAD_DIM)
    g_l = p_l[..., qd + 2 * kvd:]
    q_l = _rope_2d(q_l, row, col)
    k_l = _rope_2d(k_l, row, col)
    b, _, l, _ = q_l.shape
    q_l = q_l.reshape(b, ATTN_KV_HEADS, ATTN_GROUP, l, ATTN_HEAD_DIM)
    p_c = h_ctx @ (w_in if with_ctx_out else w_in[:, qd:qd + 2 * kvd])
    off = qd if with_ctx_out else 0
    k_c = _rms_norm(_heads(p_c[..., off:off + kvd], ATTN_KV_HEADS, ATTN_HEAD_DIM), k_gain)
    v_c = _heads(p_c[..., off + kvd:off + 2 * kvd], ATTN_KV_HEADS, ATTN_HEAD_DIM)
    n_ctx = k_c.shape[2]

    nb = l // ATTN_BLOCK
    qb = q_l.reshape(b, ATTN_KV_HEADS, ATTN_GROUP, nb, ATTN_BLOCK, ATTN_HEAD_DIM)

    def band(t):
        tp = jnp.pad(t, ((0, 0), (0, 0), (ATTN_BLOCK, ATTN_BLOCK), (0, 0)))
        tp = tp.reshape(b, ATTN_KV_HEADS, nb + 2, ATTN_BLOCK, ATTN_HEAD_DIM)
        return jnp.concatenate([tp[:, :, :-2], tp[:, :, 1:-1], tp[:, :, 2:]], axis=3)

    kb, vb = band(k_l), band(v_l)
    s_win = jnp.einsum('bkgnid,bknjd->bkgnij', qb, kb).astype(jnp.float32) * scale
    s_ctx = jnp.einsum('bkgnid,bkcd->bkgnic', qb, k_c).astype(jnp.float32) * scale
    i_idx = jnp.arange(ATTN_BLOCK)[None, :, None]
    j_idx = jnp.arange(3 * ATTN_BLOCK)[None, None, :]
    n_idx = jnp.arange(nb)[:, None, None]
    q_pos = n_idx * ATTN_BLOCK + i_idx
    k_pos = n_idx * ATTN_BLOCK - ATTN_BLOCK + j_idx
    valid = (jnp.abs(q_pos - k_pos) <= WINDOW) & (k_pos >= 0) & (k_pos < l)
    s_win = jnp.where(valid, s_win, NEG_INF)
    sink_col = jnp.broadcast_to(sink_g[None, :, :, None, None, None], s_win.shape[:-1] + (1,))
    probs = jax.nn.softmax(jnp.concatenate([s_win, s_ctx, sink_col], axis=-1), axis=-1)
    p_win = probs[..., :3 * ATTN_BLOCK].astype(vb.dtype)
    p_ctx = probs[..., 3 * ATTN_BLOCK:3 * ATTN_BLOCK + n_ctx].astype(v_c.dtype)
    o_l = (jnp.einsum('bkgnij,bknjd->bkgnid', p_win, vb)
           + jnp.einsum('bkgnic,bkcd->bkgnid', p_ctx, v_c))
    o_l = o_l.reshape(b, ATTN_KV_HEADS, ATTN_GROUP, l, ATTN_HEAD_DIM)
    y_l = _attn_out(o_l, g_l, w_out)

    y_c = None
    if with_ctx_out:
        q_c = _rms_norm(_heads(p_c[..., :qd], ATTN_HEADS, ATTN_HEAD_DIM), q_gain)
        q_c = q_c.reshape(b, ATTN_KV_HEADS, ATTN_GROUP, n_ctx, ATTN_HEAD_DIM)
        s_c = jnp.einsum('bkgid,bkcd->bkgic', q_c, k_c).astype(jnp.float32) * scale
        sink_c = jnp.broadcast_to(sink_g[None, :, :, None, None], s_c.shape[:-1] + (1,))
        p_c_attn = jax.nn.softmax(jnp.concatenate([s_c, sink_c], axis=-1), axis=-1)[..., :n_ctx]
        o_c = jnp.einsum('bkgic,bkcd->bkgid', p_c_attn.astype(v_c.dtype), v_c)
        y_c = _attn_out(o_c, p_c[..., qd + 2 * kvd:], w_out)
    return y_l, y_c


def setup_inputs(seed: int = 0) -> dict:
    key = jax.random.key(seed)
    ks = jax.random.split(key, 20)
    f32 = jnp.float32

    def nrm(k, shape, s):
        return jax.random.normal(k, shape, f32) * s

    e = 5.0 + jnp.arange(RET_HEADS, dtype=f32)
    gamma_logit = jnp.log1p(-jnp.exp2(-e)) + e * math.log(2.0)
    return {
        'x': nrm(ks[0], (BATCH, SEQ, D_MODEL), 1.0),
        'c': nrm(ks[1], (BATCH, D_MODEL), 1.0),
        'ctx': nrm(ks[2], (BATCH, CTX_LEN, D_MODEL), 1.0),
        'c_ctx': nrm(ks[3], (D_MODEL,), 1.0),
        'norm_gain': 1.0 + nrm(ks[4], (DEPTH, D_MODEL), 0.1),
        'ada_w': nrm(ks[5], (DEPTH, D_MODEL, 3 * D_MODEL), 0.5 * D_MODEL ** -0.5),
        'ada_b': nrm(ks[6], (DEPTH, 3 * D_MODEL), 0.01),
        'ret_w_in': nrm(ks[7], (N_RET_LAYERS, D_MODEL, RET_IN), D_MODEL ** -0.5),
        'ret_decay_logit_fwd': gamma_logit[None] + nrm(ks[8], (N_RET_LAYERS, RET_HEADS), 0.1),
        'ret_decay_logit_bwd': gamma_logit[None] + nrm(ks[9], (N_RET_LAYERS, RET_HEADS), 0.1),
        'ret_norm_gain': 1.0 + nrm(ks[10], (N_RET_LAYERS, RET_V_W), 0.1),
        'ret_w_out': nrm(ks[11], (N_RET_LAYERS, RET_V_W, D_MODEL), RET_V_W ** -0.5),
        'attn_w_in': nrm(ks[12], (N_ATTN_LAYERS, D_MODEL, ATTN_IN), D_MODEL ** -0.5),
        'attn_q_gain': 1.0 + nrm(ks[13], (N_ATTN_LAYERS, ATTN_HEAD_DIM), 0.1),
        'attn_k_gain': 1.0 + nrm(ks[14], (N_ATTN_LAYERS, ATTN_HEAD_DIM), 0.1),
        'attn_sink': nrm(ks[15], (N_ATTN_LAYERS, ATTN_HEADS), 0.5),
        'attn_w_out': nrm(ks[16], (N_ATTN_LAYERS, ATTN_Q_W, D_MODEL), ATTN_Q_W ** -0.5),
    }


def reference(x, c, ctx, c_ctx, norm_gain, ada_w, ada_b, ret_w_in, ret_decay_logit_fwd,
              ret_decay_logit_bwd, ret_norm_gain, ret_w_out, attn_w_in, attn_q_gain,
              attn_k_gain, attn_sink, attn_w_out):
    l = x.shape[1]
    rows = l // GRID_W
    row = jnp.repeat(jnp.arange(rows, dtype=jnp.int32), GRID_W)
    col = jnp.tile(jnp.arange(GRID_W, dtype=jnp.int32), rows)
    c_act = jax.nn.silu(c)
    cc_act = jax.nn.silu(c_ctx)
    for i in range(DEPTH):
        j = i // N_MIXERS
        with_ctx_out = i < DEPTH - 1
        mod_l = c_act @ ada_w[i] + ada_b[i]
        mod_c = cc_act @ ada_w[i] + ada_b[i]
        sh_l, sc_l, gt_l = jnp.split(mod_l, 3, axis=-1)
        sh_c, sc_c, gt_c = jnp.split(mod_c, 3, axis=-1)
        h_l = _rms_norm(x, norm_gain[i]) * (1 + sc_l[:, None]) + sh_l[:, None]
        h_c = _rms_norm(ctx, norm_gain[i]) * (1 + sc_c) + sh_c
        if i % N_MIXERS == 0:
            y_l, y_c = _retention_mixer(h_l, h_c, ret_w_in[j], ret_decay_logit_fwd[j],
                                        ret_decay_logit_bwd[j], ret_norm_gain[j], ret_w_out[j],
                                        row, col, with_ctx_out)
        else:
            y_l, y_c = _attention_mixer(h_l, h_c, attn_w_in[j], attn_q_gain[j], attn_k_gain[j],
                                        attn_sink[j], attn_w_out[j], row, col, with_ctx_out)
        x = x + gt_l[:, None] * y_l
        if with_ctx_out:
            ctx = ctx + gt_c * y_c
    return x
```

```python
import functools

import jax
import jax.numpy as jnp
from jax import lax
from jax.experimental import pallas as pl
from jax.experimental.pallas import tpu as pltpu

GRID_W = 64
RET_DK = 256
RET_DV = 512
RET_CHUNK = 128
ATTN_DH = 128
ATTN_GROUP = 4
ATTN_BLOCK = 128
ROPE_BASE = 10000.0
EPS = 1e-6
NEG_INF = -1e30
MOD_ROWS = 8
VMEM_LIMIT = 56 * 1024 * 1024

F32 = jnp.float32
BF16 = jnp.bfloat16


def _tile(n, pref):
    t = min(n, pref)
    while n % t:
        t //= 2
    return t


def _dot(a, b):
    return jnp.dot(a, b, preferred_element_type=F32)


def _dot_nt(a, b):
    return lax.dot_general(a, b, (((1,), (1,)), ((), ())), preferred_element_type=F32)


def _dot_tn(a, b):
    return lax.dot_general(a, b, (((0,), (0,)), ((), ())), preferred_element_type=F32)


def _silu(x):
    return x * jax.nn.sigmoid(x)


def _rms(x, gain):
    return x * lax.rsqrt(jnp.mean(x * x, axis=-1, keepdims=True) + EPS) * gain


def _mod_kernel(cc_ref, w_ref, b_ref, o_ref):
    a = _silu(cc_ref[...]).astype(BF16)
    o_ref[...] = _dot(a, w_ref[...].astype(BF16)) + b_ref[...]


def _modulation(cc, ada_w, ada_b):
    depth, d, n = ada_w.shape
    tn = _tile(n, 1024)
    return pl.pallas_call(
        _mod_kernel,
        out_shape=jax.ShapeDtypeStruct((depth, MOD_ROWS, n), F32),
        grid=(depth, n // tn),
        in_specs=[
            pl.BlockSpec((MOD_ROWS, d), lambda l, j: (0, 0)),
            pl.BlockSpec((None, d, tn), lambda l, j: (l, 0, j)),
            pl.BlockSpec((None, 1, tn), lambda l, j: (l, 0, j)),
        ],
        out_specs=pl.BlockSpec((None, MOD_ROWS, tn), lambda l, j: (l, 0, j)),
        compiler_params=pltpu.CompilerParams(
            dimension_semantics=("arbitrary", "arbitrary"), vmem_limit_bytes=VMEM_LIMIT),
        name="modulation",
    )(cc, ada_w, ada_b.reshape(depth, 1, n))


def _norm_mod_kernel(x_ref, gain_ref, sc_ref, sh_ref, o_ref):
    y = _rms(x_ref[...], gain_ref[...])
    o_ref[...] = (y * (1.0 + sc_ref[...]) + sh_ref[...]).astype(o_ref.dtype)


def _norm_mod(x, gain, sc, sh):
    nb, r, d = x.shape
    tr = _tile(r, 512)
    return pl.pallas_call(
        _norm_mod_kernel,
        out_shape=jax.ShapeDtypeStruct((nb, r, d), BF16),
        grid=(nb, r // tr),
        in_specs=[
            pl.BlockSpec((None, tr, d), lambda b, i: (b, i, 0)),
            pl.BlockSpec((1, d), lambda b, i: (0, 0)),
            pl.BlockSpec((None, 1, d), lambda b, i: (b, 0, 0)),
            pl.BlockSpec((None, 1, d), lambda b, i: (b, 0, 0)),
        ],
        out_specs=pl.BlockSpec((None, tr, d), lambda b, i: (b, i, 0)),
        compiler_params=pltpu.CompilerParams(
            dimension_semantics=("arbitrary", "arbitrary"), vmem_limit_bytes=VMEM_LIMIT),
        name="norm_mod",
    )(x, gain, sc, sh)


def _in_proj_kernel(a_ref, w_ref, o_ref, wb_ref):
    @pl.when(pl.program_id(1) == 0)
    def _():
        wb_ref[...] = w_ref[...].astype(BF16)

    o_ref[...] = _dot(a_ref[...], wb_ref[...]).astype(o_ref.dtype)


def _in_proj(a, w_all, layer, col0=0, ncols=None):
    m, k = a.shape
    n = w_all.shape[2] if ncols is None else ncols
    tm = _tile(m, 1024)
    tn = _tile(n, 1024)
    assert col0 % tn == 0
    jb = col0 // tn
    return pl.pallas_call(
        _in_proj_kernel,
        out_shape=jax.ShapeDtypeStruct((m, n), BF16),
        grid=(n // tn, m // tm),
        in_specs=[
            pl.BlockSpec((tm, k), lambda j, i: (i, 0)),
            pl.BlockSpec((None, k, tn), lambda j, i: (layer, 0, jb + j)),
        ],
        out_specs=pl.BlockSpec((tm, tn), lambda j, i: (i, j)),
        scratch_shapes=[pltpu.VMEM((k, tn), BF16)],
        compiler_params=pltpu.CompilerParams(
            dimension_semantics=("arbitrary", "arbitrary"), vmem_limit_bytes=VMEM_LIMIT),
        name="in_proj",
    )(a, w_all)


def _out_proj_kernel(a_ref, w_ref, x_ref, gt_ref, o_ref, wb_ref):
    @pl.when((pl.program_id(1) == 0) & (pl.program_id(2) == 0))
    def _():
        wb_ref[...] = w_ref[...].astype(BF16)

    o_ref[...] = x_ref[...] + gt_ref[...] * _dot(a_ref[...], wb_ref[...])


def _out_proj(a, w_all, layer, x, gt):
    nb, r, k = a.shape
    d = x.shape[2]
    tm = _tile(r, 512)
    tn = _tile(d, 512)
    return pl.pallas_call(
        _out_proj_kernel,
        out_shape=jax.ShapeDtypeStruct((nb, r, d), F32),
        grid=(d // tn, nb, r // tm),
        in_specs=[
            pl.BlockSpec((None, tm, k), lambda j, b, i: (b, i, 0)),
            pl.BlockSpec((None, k, tn), lambda j, b, i: (layer, 0, j)),
            pl.BlockSpec((None, tm, tn), lambda j, b, i: (b, i, j)),
            pl.BlockSpec((None, 1, tn), lambda j, b, i: (b, 0, j)),
        ],
        out_specs=pl.BlockSpec((None, tm, tn), lambda j, b, i: (b, i, j)),
        scratch_shapes=[pltpu.VMEM((k, tn), BF16)],
        compiler_params=pltpu.CompilerParams(
            dimension_semantics=("arbitrary", "arbitrary", "arbitrary"),
            vmem_limit_bytes=VMEM_LIMIT),
        name="out_proj",
    )(a, w_all, x, gt)


def _rope_tables(l, head_dim):
    n2 = head_dim // 4
    inv = ROPE_BASE ** (-jnp.arange(n2, dtype=F32) / n2)
    t = jnp.arange(l, dtype=jnp.int32)
    row = (t // GRID_W).astype(F32)
    col = (t % GRID_W).astype(F32)
    ar = row[:, None] * inv[None, :]
    ac = col[:, None] * inv[None, :]
    cr, sr, cc, sc = jnp.cos(ar), jnp.sin(ar), jnp.cos(ac), jnp.sin(ac)
    return (jnp.concatenate([cr, cr, cc, cc], axis=-1),
            jnp.concatenate([-sr, sr, -sc, sc], axis=-1))


def _rope256(x, cos, sin):
    xr = jnp.concatenate(
        [pltpu.roll(x[:, g * 128:(g + 1) * 128], 64, 1) for g in range(x.shape[1] // 128)], axis=1)
    return x * cos + xr * sin


def _rope128(x, cos, sin):
    lane = lax.broadcasted_iota(jnp.int32, (1, 128), 1)
    first = (lane % 64) < 32
    xr = jnp.where(first, pltpu.roll(x, 96, 1), pltpu.roll(x, 32, 1))
    return x * cos + xr * sin


def _ret_core_kernel(lf_ref, lb_ref, ql_ref, kl_ref, vl_ref, gl_ref, qc_ref, kc_ref, vc_ref,
                     gc_ref, cos_ref, sin_ref, ng_ref, ol_ref, oc_ref,
                     qr_s, kr_s, acc_s, sf_s, sb_s, *, chunk):
    c_sz = chunk
    n_lat = ql_ref.shape[0] // c_sz
    n_ctx = qc_ref.shape[0] // c_sz
    scale = RET_DK ** -0.5

    lgf = jax.nn.log_sigmoid(lf_ref[...])
    lgb = jax.nn.log_sigmoid(lb_ref[...])
    ii = lax.broadcasted_iota(jnp.int32, (c_sz, c_sz), 0).astype(F32)
    jj = lax.broadcasted_iota(jnp.int32, (c_sz, c_sz), 1).astype(F32)
    diff = ii - jj
    dmat = jnp.where(diff >= 0, jnp.exp(lgf * jnp.maximum(diff, 0.0)),
                     jnp.exp(lgb * jnp.maximum(-diff, 0.0))) * scale
    idx = lax.broadcasted_iota(jnp.int32, (c_sz, 1), 0).astype(F32)
    qwf = jnp.exp(lgf * (idx + 1.0)) * scale
    kwf = jnp.exp(lgf * (c_sz - 1.0 - idx))
    cdf = jnp.exp(lgf * float(c_sz))
    qwb = jnp.exp(lgb * (c_sz - idx)) * scale
    kwb = jnp.exp(lgb * idx)
    cdb = jnp.exp(lgb * float(c_sz))
    ng = ng_ref[...]

    rr = 256

    def rope_body(t, carry):
        r0 = pl.multiple_of(t * rr, rr)
        cs = cos_ref[pl.ds(r0, rr), :]
        sn = sin_ref[pl.ds(r0, rr), :]
        qr_s[pl.ds(r0, rr), :] = _rope256(ql_ref[pl.ds(r0, rr), :].astype(F32), cs, sn).astype(BF16)
        kr_s[pl.ds(r0, rr), :] = _rope256(kl_ref[pl.ds(r0, rr), :].astype(F32), cs, sn).astype(BF16)
        return carry

    lax.fori_loop(0, ql_ref.shape[0] // rr, rope_body, 0)

    sf_s[...] = jnp.zeros_like(sf_s)
    sb_s[...] = jnp.zeros_like(sb_s)

    def scan(q_ref, k_ref, v_ref, g_ref, o_ref, n):
        def fwd(c, carry):
            r0 = pl.multiple_of(c * c_sz, c_sz)
            q = q_ref[pl.ds(r0, c_sz), :]
            k = k_ref[pl.ds(r0, c_sz), :]
            v = v_ref[pl.ds(r0, c_sz), :]
            a = (_dot_nt(q, k) * dmat).astype(BF16)
            cross = _dot(q, sf_s[...].astype(BF16)) * qwf
            acc_s[pl.ds(r0, c_sz), :] = _dot(a, v) + cross
            kw = (k.astype(F32) * kwf).astype(BF16)
            sf_s[...] = cdf * sf_s[...] + _dot_tn(kw, v)
            return carry

        lax.fori_loop(0, n, fwd, 0)

        def bwd(t, carry):
            r0 = pl.multiple_of((n - 1 - t) * c_sz, c_sz)
            q = q_ref[pl.ds(r0, c_sz), :]
            k = k_ref[pl.ds(r0, c_sz), :]
            v = v_ref[pl.ds(r0, c_sz), :]
            o = acc_s[pl.ds(r0, c_sz), :] + _dot(q, sb_s[...].astype(BF16)) * qwb
            g = g_ref[pl.ds(r0, c_sz), :].astype(F32)
            o_ref[pl.ds(r0, c_sz), :] = (_rms(o, ng) * _silu(g)).astype(o_ref.dtype)
            kw = (k.astype(F32) * kwb).astype(BF16)
            sb_s[...] = cdb * sb_s[...] + _dot_tn(kw, v)
            return carry

        lax.fori_loop(0, n, bwd, 0)

    scan(qc_ref, kc_ref, vc_ref, gc_ref, oc_ref, n_ctx)
    scan(qr_s, kr_s, vl_ref, gl_ref, ol_ref, n_lat)


def _ret_core(p_l, p_c, logit_f, logit_b, norm_gain, cos, sin):
    nb, l, n = p_l.shape
    lc = p_c.shape[1]
    heads = n // (2 * RET_DK + 2 * RET_DV)
    k_off = heads
    v_off = 2 * heads * RET_DK // RET_DV
    g_off = v_off + heads

    def col(width, off, rows):
        return pl.BlockSpec((None, rows, width), lambda b, h: (b, 0, off + h))

    per_head = pl.BlockSpec((None, 1, 1), lambda b, h: (h, 0, 0))
    return pl.pallas_call(
        functools.partial(_ret_core_kernel, chunk=RET_CHUNK),
        out_shape=(jax.ShapeDtypeStruct((nb, l, heads * RET_DV), BF16),
                   jax.ShapeDtypeStruct((nb, lc, heads * RET_DV), BF16)),
        grid=(nb, heads),
        in_specs=[
            per_head, per_head,
            col(RET_DK, 0, l), col(RET_DK, k_off, l), col(RET_DV, v_off, l), col(RET_DV, g_off, l),
            col(RET_DK, 0, lc), col(RET_DK, k_off, lc), col(RET_DV, v_off, lc), col(RET_DV, g_off, lc),
            pl.BlockSpec((l, RET_DK), lambda b, h: (0, 0)),
            pl.BlockSpec((l, RET_DK), lambda b, h: (0, 0)),
            pl.BlockSpec((1, RET_DV), lambda b, h: (0, h)),
        ],
        out_specs=(pl.BlockSpec((None, l, RET_DV), lambda b, h: (b, 0, h)),
                   pl.BlockSpec((None, lc, RET_DV), lambda b, h: (b, 0, h))),
        scratch_shapes=[
            pltpu.VMEM((l, RET_DK), BF16), pltpu.VMEM((l, RET_DK), BF16),
            pltpu.VMEM((l, RET_DV), F32),
            pltpu.VMEM((RET_DK, RET_DV), F32), pltpu.VMEM((RET_DK, RET_DV), F32),
        ],
        compiler_params=pltpu.CompilerParams(
            dimension_semantics=("arbitrary", "arbitrary"), vmem_limit_bytes=VMEM_LIMIT),
        name="ret_core",
    )(logit_f.reshape(heads, 1, 1), logit_b.reshape(heads, 1, 1),
      p_l, p_l, p_l, p_l, p_c, p_c, p_c, p_c, cos, sin, norm_gain.reshape(1, heads * RET_DV))


def _softmax_pv(parts, sink_col):
    m = sink_col
    for s, _ in parts:
        m = jnp.maximum(m, jnp.max(s, axis=-1, keepdims=True))
    den = jnp.exp(sink_col - m)
    acc = None
    for s, v in parts:
        p = jnp.exp(s - m)
        den = den + jnp.sum(p, axis=-1, keepdims=True)
        pv = _dot(p.astype(BF16), v)
        acc = pv if acc is None else acc + pv
    return acc / den


def _attn_core_kernel(*refs, with_ctx):
    if with_ctx:
        (sink_ref, q_ref, k_ref, v_ref, g_ref, kc_ref, vc_ref, qc_ref, gc_ref, cos_ref, sin_ref,
         qg_ref, kg_ref, o_ref, oc_ref, kp_s, vp_s, kcn_s) = refs
    else:
        (sink_ref, q_ref, k_ref, v_ref, g_ref, kc_ref, vc_ref, cos_ref, sin_ref,
         qg_ref, kg_ref, o_ref, kp_s, vp_s, kcn_s) = refs
    blk = ATTN_BLOCK
    l = q_ref.shape[0]
    nblk = l // blk
    scale = ATTN_DH ** -0.5
    kk = pl.program_id(1)
    qg = qg_ref[...]
    kg = kg_ref[...]

    zeros = jnp.zeros((blk, ATTN_DH), BF16)
    kp_s[pl.ds(0, blk), :] = zeros
    vp_s[pl.ds(0, blk), :] = zeros
    kp_s[pl.ds(l + blk, blk), :] = zeros
    vp_s[pl.ds(l + blk, blk), :] = zeros
    rr = 256

    def prep(t, carry):
        r0 = pl.multiple_of(t * rr, rr)
        kn = _rms(k_ref[pl.ds(r0, rr), :].astype(F32), kg)
        kn = _rope128(kn, cos_ref[pl.ds(r0, rr), :], sin_ref[pl.ds(r0, rr), :])
        kp_s[pl.ds(blk + r0, rr), :] = kn.astype(BF16)
        vp_s[pl.ds(blk + r0, rr), :] = v_ref[pl.ds(r0, rr), :]
        return carry

    lax.fori_loop(0, l // rr, prep, 0)
    kcn_s[...] = _rms(kc_ref[...].astype(F32), kg).astype(BF16)
    vc = vc_ref[...]

    def sink_rows(rows):
        return jnp.concatenate(
            [jnp.full((rows, 1), sink_ref[kk * ATTN_GROUP + g], F32) for g in range(ATTN_GROUP)],
            axis=0)

    i_idx = lax.broadcasted_iota(jnp.int32, (blk, 3 * blk), 0)
    j_idx = lax.broadcasted_iota(jnp.int32, (blk, 3 * blk), 1)
    in_window = (j_idx >= i_idx) & (j_idx <= i_idx + 2 * blk)
    sink_q = sink_rows(blk)

    def qblock(n, carry):
        r0 = pl.multiple_of(n * blk, blk)
        cs = cos_ref[pl.ds(r0, blk), :]
        sn = sin_ref[pl.ds(r0, blk), :]
        qs = jnp.concatenate(
            [_rope128(_rms(q_ref[pl.ds(r0, blk), g * ATTN_DH:(g + 1) * ATTN_DH].astype(F32), qg),
                      cs, sn).astype(BF16) for g in range(ATTN_GROUP)], axis=0)
        kb = kp_s[pl.ds(r0, 3 * blk), :]
        vb = vp_s[pl.ds(r0, 3 * blk), :]
        k_pos = r0 - blk + j_idx
        valid = in_window & (k_pos >= 0) & (k_pos < l)
        valid = jnp.concatenate([valid] * ATTN_GROUP, axis=0)
        s_win = jnp.where(valid, _dot_nt(qs, kb) * scale, NEG_INF)
        s_ctx = _dot_nt(qs, kcn_s[...]) * scale
        o = _softmax_pv([(s_win, vb), (s_ctx, vc)], sink_q)
        for g in range(ATTN_GROUP):
            gate = g_ref[pl.ds(r0, blk), g * ATTN_DH:(g + 1) * ATTN_DH].astype(F32)
            o_ref[pl.ds(r0, blk), g * ATTN_DH:(g + 1) * ATTN_DH] = (
                o[g * blk:(g + 1) * blk] * _silu(gate)).astype(o_ref.dtype)
        return carry

    lax.fori_loop(0, nblk, qblock, 0)

    if with_ctx:
        lc = qc_ref.shape[0]
        qs = jnp.concatenate(
            [_rms(qc_ref[:, g * ATTN_DH:(g + 1) * ATTN_DH].astype(F32), qg).astype(BF16)
             for g in range(ATTN_GROUP)], axis=0)
        s = _dot_nt(qs, kcn_s[...]) * scale
        o = _softmax_pv([(s, vc)], sink_rows(lc))
        for g in range(ATTN_GROUP):
            gate = gc_ref[:, g * ATTN_DH:(g + 1) * ATTN_DH].astype(F32)
            oc_ref[:, g * ATTN_DH:(g + 1) * ATTN_DH] = (
                o[g * lc:(g + 1) * lc] * _silu(gate)).astype(oc_ref.dtype)


def _attn_core(p_l, p_c, sink, q_gain, k_gain, cos, sin, with_ctx):
    nb, l, n = p_l.shape
    lc = p_c.shape[1]
    d = n * 2 // 5
    kvh = d // (ATTN_DH * ATTN_GROUP)
    gw = ATTN_GROUP * ATTN_DH
    k_off = d // ATTN_DH
    v_off = k_off + kvh
    g_off = (d + d // 2) // gw
    kc_off, vc_off = (k_off, v_off) if with_ctx else (0, kvh)

    def col(width, off, rows):
        return pl.BlockSpec((None, rows, width), lambda b, k: (b, 0, off + k))

    whole = lambda rows: pl.BlockSpec((rows, ATTN_DH), lambda b, k: (0, 0))
    in_specs = [
        pl.BlockSpec(memory_space=pltpu.SMEM),
        col(gw, 0, l), col(ATTN_DH, k_off, l), col(ATTN_DH, v_off, l), col(gw, g_off, l),
        col(ATTN_DH, kc_off, lc), col(ATTN_DH, vc_off, lc),
    ]
    args = [sink, p_l, p_l, p_l, p_l, p_c, p_c]
    out_shape = [jax.ShapeDtypeStruct((nb, l, d), BF16)]
    out_specs = [pl.BlockSpec((None, l, gw), lambda b, k: (b, 0, k))]
    if with_ctx:
        in_specs += [col(gw, 0, lc), col(gw, g_off, lc)]
        args += [p_c, p_c]
        out_shape.append(jax.ShapeDtypeStruct((nb, lc, d), BF16))
        out_specs.append(pl.BlockSpec((None, lc, gw), lambda b, k: (b, 0, k)))
    in_specs += [whole(l), whole(l), whole(1), whole(1)]
    args += [cos, sin, q_gain.reshape(1, ATTN_DH), k_gain.reshape(1, ATTN_DH)]
    outs = pl.pallas_call(
        functools.partial(_attn_core_kernel, with_ctx=with_ctx),
        out_shape=tuple(out_shape),
        grid=(nb, kvh),
        in_specs=in_specs,
        out_specs=tuple(out_specs),
        scratch_shapes=[
            pltpu.VMEM((l + 2 * ATTN_BLOCK, ATTN_DH), BF16),
            pltpu.VMEM((l + 2 * ATTN_BLOCK, ATTN_DH), BF16),
            pltpu.VMEM((lc, ATTN_DH), BF16),
        ],
        compiler_params=pltpu.CompilerParams(
            dimension_semantics=("arbitrary", "arbitrary"), vmem_limit_bytes=VMEM_LIMIT),
        name="attn_core",
    )(*args)
    return (outs[0], outs[1]) if with_ctx else (outs[0], None)


def kernel(x, c, ctx, c_ctx, norm_gain, ada_w, ada_b, ret_w_in, ret_decay_logit_fwd,
           ret_decay_logit_bwd, ret_norm_gain, ret_w_out, attn_w_in, attn_q_gain, attn_k_gain,
           attn_sink, attn_w_out):
    nb, l, d = x.shape
    lc = ctx.shape[1]
    depth = norm_gain.shape[0]
    assert nb + 1 <= MOD_ROWS
    cc = jnp.concatenate([c, c_ctx[None], jnp.zeros((MOD_ROWS - nb - 1, d), F32)], axis=0)
    mod = _modulation(cc, ada_w, ada_b)
    cos_r, sin_r = _rope_tables(l, RET_DK)
    cos_a, sin_a = _rope_tables(l, ATTN_DH)

    for i in range(depth):
        j = i // 2
        with_ctx = i < depth - 1
        sh_l, sc_l, gt_l = (mod[i, :nb, s * d:(s + 1) * d][:, None] for s in range(3))
        sh_c, sc_c, gt_c = (jnp.broadcast_to(mod[i, nb, s * d:(s + 1) * d], (nb, 1, d))
                            for s in range(3))
        gain = norm_gain[i][None]
        h_l = _norm_mod(x, gain, sc_l, sh_l).reshape(nb * l, d)
        h_c = _norm_mod(ctx, gain, sc_c, sh_c).reshape(nb * lc, d)
        if i % 2 == 0:
            p_l = _in_proj(h_l, ret_w_in, j).reshape(nb, l, -1)
            p_c = _in_proj(h_c, ret_w_in, j).reshape(nb, lc, -1)
            og_l, og_c = _ret_core(p_l, p_c, ret_decay_logit_fwd[j], ret_decay_logit_bwd[j],
                                   ret_norm_gain[j], cos_r, sin_r)
            w_out = ret_w_out
        else:
            p_l = _in_proj(h_l, attn_w_in, j).reshape(nb, l, -1)
            if with_ctx:
                p_c = _in_proj(h_c, attn_w_in, j).reshape(nb, lc, -1)
            else:
                p_c = _in_proj(h_c, attn_w_in, j, col0=d, ncols=d // 2).reshape(nb, lc, -1)
            og_l, og_c = _attn_core(p_l, p_c, attn_sink[j], attn_q_gain[j], attn_k_gain[j],
                                    cos_a, sin_a, with_ctx)
            w_out = attn_w_out
        x = _out_proj(og_l, w_out, j, x, gt_l)
        if with_ctx:
            ctx = _out_proj(og_c, w_out, j, ctx, gt_c)
    return x
```
